```python
import jax, jax.numpy as jnp
from jax import lax
import numpy as np

D_MODEL = 1024
BATCH = 32
SEQ = 256
DEPTH = 1
DEC_BATCH = 2
DEC_SEQ = 4096
PAST_LEN = 512

GRID_W = 64
CHUNK = 128
D_G = 1024
G_HEADS = 8
G_DIM = D_G // G_HEADS
D_RNN = 1024
LRU_HEADS = 16
LRU_DIM = D_RNN // LRU_HEADS
CONV_W = 4
CONV_PAD_L = 2
LRU_C = 8.0
D_FF = 4 * D_MODEL
N_MOD = 6
EPS = 1e-6
SPLITS = [D_G, 2 * D_G, 2 * D_G + D_RNN, 2 * D_G + 2 * D_RNN, 2 * D_G + 2 * D_RNN + D_MODEL]
IN_COLS = 2 * D_G + 2 * D_RNN + 2 * D_MODEL

kernel_name = "hybrid_gmlp_rglru_diffusion_step"


def _rmsnorm(x, g):
    xf = x.astype(jnp.float32)
    y = xf * lax.rsqrt(jnp.mean(xf * xf, axis=-1, keepdims=True) + EPS)
    return (y * g.astype(jnp.float32)).astype(x.dtype)


def _centred_dwconv(x, w, b):
    L = x.shape[1]
    xp = jnp.pad(x, ((0, 0), (CONV_PAD_L, CONV_W - 1 - CONV_PAD_L), (0, 0)))
    y = xp[:, 0:L] * w[0]
    for k in range(1, CONV_W):
        y = y + xp[:, k:k + L] * w[k]
    return y + b


def _lru_combine(e1, e2):
    a1, b1 = e1
    a2, b2 = e2
    return a1 * a2, a2 * b1 + b2


def _scan_dir(a, bx, h0, reverse):
    a_cum, h_zero = lax.associative_scan(_lru_combine, (a, bx), axis=1, reverse=reverse)
    return h_zero + a_cum * h0[:, None, :]


def _sgu(u, v, g_sgu, w_sp, b_sp):
    B, L, _ = v.shape
    vn = _rmsnorm(v, g_sgu).reshape(B, L // CHUNK, CHUNK, G_HEADS, G_DIM)
    s = jnp.einsum('gqp,bnpgc->bnqgc', w_sp, vn) + b_sp.T[None, None, :, :, None]
    return u * s.reshape(B, L, D_G)


def _rglru(xr, h0_f, h0_b, conv_w, conv_b, w_ra, b_ra, w_ri, b_ri, lam):
    B, L, _ = xr.shape
    xc = _centred_dwconv(xr, conv_w, conv_b)
    xh = xc.reshape(B, L, LRU_HEADS, LRU_DIM)
    r = jax.nn.sigmoid((jnp.einsum('blhi,dhij->bldhj', xh, w_ra).reshape(B, L, 2, D_RNN) + b_ra).astype(jnp.float32))
    ig = jax.nn.sigmoid((jnp.einsum('blhi,dhij->bldhj', xh, w_ri).reshape(B, L, 2, D_RNN) + b_ri).astype(jnp.float32))
    log_a = -LRU_C * r * jax.nn.softplus(-lam.astype(jnp.float32))
    a = jnp.exp(log_a)
    bx = jnp.sqrt(jnp.maximum(-jnp.expm1(2.0 * log_a), 1e-12)) * ig * xc.astype(jnp.float32)[:, :, None, :]
    h_f = _scan_dir(a[:, :, 0], bx[:, :, 0], h0_f, False)
    h_b = _scan_dir(a[:, :, 1], bx[:, :, 1], h0_b, True)
    return h_f + h_b, h_f[:, -1], h_b[:, 0]


def _layer(x, c_vec, h0_f, h0_b, p):
    mod = (jax.nn.silu(c_vec) @ p['w_ada'] + p['b_ada'])[:, None, :]
    sh1, sc1, g1, sh2, sc2, g2 = jnp.split(mod, N_MOD, axis=-1)
    h = _rmsnorm(x, p['g_pre_mix']) * (1.0 + sc1) + sh1
    z = h @ p['w_in']
    u, v, xr, gr, ga, gb = jnp.split(z, SPLITS, axis=-1)
    y_g = _sgu(jax.nn.gelu(u), jax.nn.gelu(v), p['g_sgu'], p['w_sp'], p['b_sp'])
    h_rnn, hf, hb = _rglru(xr, h0_f, h0_b, p['conv_w'], p['conv_b'], p['w_ra'], p['b_ra'],
                           p['w_ri'], p['b_ri'], p['lam'])
    y_r = h_rnn.astype(x.dtype) * jax.nn.gelu(gr)
    merged = jax.nn.sigmoid(ga) * (y_g @ p['w_br_g']) + jax.nn.sigmoid(gb) * (y_r @ p['w_br_r'])
    x = x + g1 * _rmsnorm(merged @ p['w_out'], p['g_post_mix'])
    h = _rmsnorm(x, p['g_pre_mlp']) * (1.0 + sc2) + sh2
    f = jnp.square(jax.nn.relu(h @ p['w_ff1'])) @ p['w_ff2']
    x = x + g2 * _rmsnorm(f, p['g_post_mlp'])
    return x, hf, hb


def setup_inputs(seed: int = 0) -> dict:
    key = jax.random.key(seed)
    ks = jax.random.split(key, 32)
    nrm = lambda k, s, sc: jax.random.normal(k, s, jnp.float32) * sc
    p_lam = jax.random.uniform(ks[20], (DEPTH, 2, D_RNN), jnp.float32, 0.9, 0.999)
    return {
        'x_prompt': nrm(ks[0], (BATCH, SEQ, D_MODEL), 1.0),
        'x_sample': nrm(ks[1], (DEC_BATCH, DEC_SEQ, D_MODEL), 1.0),
        'state_lru': nrm(ks[2], (DEC_BATCH, DEPTH, 2, D_RNN), 0.5),
        'c': nrm(ks[3], (DEC_BATCH, D_MODEL), 1.0),
        'c_ctx': nrm(ks[4], (D_MODEL,), 1.0),
        'w_ada': nrm(ks[5], (DEPTH, D_MODEL, N_MOD * D_MODEL), 0.5 * D_MODEL ** -0.5),
        'b_ada': nrm(ks[6], (DEPTH, N_MOD * D_MODEL), 0.01),
        'g_pre_mix': 1.0 + nrm(ks[7], (DEPTH, D_MODEL), 0.05),
        'g_post_mix': 1.0 + nrm(ks[8], (DEPTH, D_MODEL), 0.05),
        'g_pre_mlp': 1.0 + nrm(ks[9], (DEPTH, D_MODEL), 0.05),
        'g_post_mlp': 1.0 + nrm(ks[10], (DEPTH, D_MODEL), 0.05),
        'w_in': nrm(ks[11], (DEPTH, D_MODEL, IN_COLS), D_MODEL ** -0.5),
        'g_sgu': 1.0 + nrm(ks[12], (DEPTH, D_G), 0.05),
        'w_sp': nrm(ks[13], (DEPTH, G_HEADS, CHUNK, CHUNK), CHUNK ** -0.5),
        'b_sp': 1.0 + nrm(ks[14], (DEPTH, G_HEADS, CHUNK), 0.05),
        'conv_w': nrm(ks[15], (DEPTH, CONV_W, D_RNN), CONV_W ** -0.5),
        'conv_b': nrm(ks[16], (DEPTH, D_RNN), 0.01),
        'w_ra': nrm(ks[17], (DEPTH, 2, LRU_HEADS, LRU_DIM, LRU_DIM), LRU_DIM ** -0.5),
        'b_ra': nrm(ks[18], (DEPTH, 2, D_RNN), 0.1),
        'w_ri': nrm(ks[19], (DEPTH, 2, LRU_HEADS, LRU_DIM, LRU_DIM), LRU_DIM ** -0.5),
        'b_ri': nrm(ks[21], (DEPTH, 2, D_RNN), 0.1),
        'lam': jnp.log(p_lam) - jnp.log1p(-p_lam),
        'w_br_g': nrm(ks[22], (DEPTH, D_G, D_MODEL), D_G ** -0.5),
        'w_br_r': nrm(ks[23], (DEPTH, D_RNN, D_MODEL), D_RNN ** -0.5),
        'w_out': nrm(ks[24], (DEPTH, D_MODEL, D_MODEL), D_MODEL ** -0.5),
        'w_ff1': nrm(ks[25], (DEPTH, D_MODEL, D_FF), D_MODEL ** -0.5),
        'w_ff2': nrm(ks[26], (DEPTH, D_FF, D_MODEL), D_FF ** -0.5),
    }


def reference(x_prompt, x_sample, state_lru, c, c_ctx, w_ada, b_ada, g_pre_mix, g_post_mix,
              g_pre_mlp, g_post_mlp, w_in, g_sgu, w_sp, b_sp, conv_w, conv_b, w_ra, b_ra,
              w_ri, b_ri, lam, w_br_g, w_br_r, w_out, w_ff1, w_ff2):
    y_prompt = x_prompt
    y_sample = x_sample
    B_ctx = x_prompt.shape[0]
    zeros_h = jnp.zeros((B_ctx, D_RNN), jnp.float32)
    ctx_states = []
    for l in range(DEPTH):
        p = {
            'w_ada': w_ada[l], 'b_ada': b_ada[l], 'g_pre_mix': g_pre_mix[l], 'g_post_mix': g_post_mix[l],
            'g_pre_mlp': g_pre_mlp[l], 'g_post_mlp': g_post_mlp[l], 'w_in': w_in[l], 'g_sgu': g_sgu[l],
            'w_sp': w_sp[l], 'b_sp': b_sp[l], 'conv_w': conv_w[l], 'conv_b': conv_b[l],
            'w_ra': w_ra[l], 'b_ra': b_ra[l], 'w_ri': w_ri[l], 'b_ri': b_ri[l], 'lam': lam[l],
            'w_br_g': w_br_g[l], 'w_br_r': w_br_r[l], 'w_out': w_out[l],
            'w_ff1': w_ff1[l], 'w_ff2': w_ff2[l],
        }
        y_prompt, hf, hb = _layer(y_prompt, c_ctx[None, :], zeros_h, zeros_h, p)
        ctx_states.append(jnp.stack([hf, hb], axis=1).astype(x_prompt.dtype))
        h0 = state_lru[:, l].astype(jnp.float32)
        y_sample, _, _ = _layer(y_sample, c, h0[:, 0], h0[:, 1], p)
    new_state_lru = jnp.stack(ctx_states, axis=1)
    return (y_prompt, y_sample, new_state_lru)
```

```python
import functools

import jax
import jax.numpy as jnp
from jax import lax
from jax.experimental import pallas as pl
from jax.experimental.pallas import tpu as pltpu

F32 = jnp.float32
BF16 = jnp.bfloat16

SUBLANES = 8
LANES = 128
BF16_ROWS = 16
VMEM_LIMIT_BYTES = 56 * 1024 * 1024

CHUNK = 128
G_HEADS = 8
LRU_HEADS = 16
LRU_DIM = 64
CONV_W = 4
CONV_PAD_L = 2
LRU_C = 8.0
N_MOD = 6
EPS = 1e-6
HALO = BF16_ROWS
HEADS_PER_GROUP = 4
GATE_GROUP = HEADS_PER_GROUP * LRU_DIM


def _sigmoid(x):
    return 0.5 * jnp.tanh(0.5 * x) + 0.5


def _gelu(x):
    return 0.5 * x * (1.0 + jnp.tanh(0.7978845608028654 * (x + 0.044715 * (x * x * x))))


def _rms_scale(x):
    return x * lax.rsqrt(jnp.mean(x * x, axis=-1, keepdims=True) + EPS)


def _dot(a, b):
    return jnp.dot(a, b, preferred_element_type=F32)


def _ada_kernel(c_ref, w_ref, b_ref, o_ref):
    c = c_ref[...]
    s = c * _sigmoid(c)
    s_hi = s.astype(BF16)
    s_lo = (s - s_hi.astype(F32)).astype(BF16)
    w = w_ref[...]
    w_hi = w.astype(BF16)
    w_lo = (w - w_hi.astype(F32)).astype(BF16)
    o_ref[...] = _dot(s_hi, w_hi) + _dot(s_lo, w_hi) + _dot(s_hi, w_lo) + b_ref[...]


def _ada(c_rows, w_ada, b_ada):
    d, n = w_ada.shape
    bn = 1536
    return pl.pallas_call(
        _ada_kernel,
        grid=(n // bn,),
        in_specs=[
            pl.BlockSpec((SUBLANES, d), lambda j: (0, 0)),
            pl.BlockSpec((d, bn), lambda j: (0, j)),
            pl.BlockSpec((1, bn), lambda j: (0, j)),
        ],
        out_specs=pl.BlockSpec((SUBLANES, bn), lambda j: (0, j)),
        out_shape=jax.ShapeDtypeStruct((SUBLANES, n), F32),
        compiler_params=pltpu.CompilerParams(
            dimension_semantics=("arbitrary",), vmem_limit_bytes=VMEM_LIMIT_BYTES),
        name="ada_mod",
    )(c_rows, w_ada, b_ada)


def _modulated_norm_ext(x_ref, xp_ref, xn_ref, g_ref, sh, sc, first, last, hext_ref):
    tm = x_ref.shape[0]
    scale = g_ref[...] * (1.0 + sc)

    def norm(x):
        return _rms_scale(x) * scale + sh

    hp = jnp.where(first, 0.0, norm(xp_ref[...]))
    hn = jnp.where(last, 0.0, norm(xn_ref[...]))
    hext_ref[0:HALO, :] = hp.astype(BF16)
    hext_ref[HALO:HALO + tm, :] = norm(x_ref[...]).astype(BF16)
    hext_ref[HALO + tm:HALO + tm + HALO, :] = hn.astype(BF16)


def _conv(xr_ref, cw_ref, cb_ref, tm):
    acc = cb_ref[...]
    for k in range(CONV_W):
        off = HALO + k - CONV_PAD_L
        acc = acc + xr_ref[off:off + tm, :] * cw_ref[k:k + 1, :]
    return acc


def _decay_log2_coeff(lam_ref):
    nl = -lam_ref[...]
    softplus = jnp.maximum(nl, 0.0) + jnp.log1p(jnp.exp(-jnp.abs(nl)))
    return -LRU_C * softplus


def _block_prefix(a, b, reverse):
    tm, n = a.shape
    a3 = a.reshape(tm // SUBLANES, SUBLANES, n)
    b3 = b.reshape(tm // SUBLANES, SUBLANES, n)
    row = lax.broadcasted_iota(jnp.int32, (1, SUBLANES, n), 1)
    for d in (1, 2, 4):
        if reverse:
            shift, valid = SUBLANES - d, row < SUBLANES - d
        else:
            shift, valid = d, row >= d
        a_sh = jnp.where(valid, pltpu.roll(a3, shift, 1), 1.0)
        b_sh = jnp.where(valid, pltpu.roll(b3, shift, 1), 0.0)
        b3 = a3 * b_sh + b3
        a3 = a3 * a_sh
    return a3.reshape(tm, n), b3.reshape(tm, n)


def _gates_to_scan_inputs(xc_ref, wg_ref, bg_ref, coef, a_ref, b_ref):
    n_groups = wg_ref.shape[0]
    for g in range(n_groups):
        cols = slice(g * GATE_GROUP, (g + 1) * GATE_GROUP)
        xc = xc_ref[:, cols]
        gate = _dot(xc.astype(BF16), wg_ref[g]) + bg_ref[g]
        for d in range(2):
            r = _sigmoid(gate[:, d * GATE_GROUP:(d + 1) * GATE_GROUP])
            ig = _sigmoid(gate[:, (2 + d) * GATE_GROUP:(3 + d) * GATE_GROUP])
            log_a = r * coef[d:d + 1, cols]
            a = jnp.exp(log_a)
            t = jnp.tanh(log_a)
            one_minus_a2 = (-2.0 * t) / (1.0 - t)
            bx = jnp.sqrt(jnp.maximum(one_minus_a2, 1e-12)) * ig * xc
            ap, bp = _block_prefix(a, bx, reverse=(d == 1))
            a_ref[d, :, cols] = ap
            b_ref[d, :, cols] = bp


def _scan_blocks(a_ref, b_ref, d, h0, h_ref):
    tm = a_ref.shape[1]
    n = a_ref.shape[2]
    nblk = tm // SUBLANES
    reverse = d == 1
    edge = 0 if reverse else SUBLANES - 1

    def body(i, carry):
        h_prev, a_prod = carry
        blk = nblk - 1 - i if reverse else i
        rows = pl.ds(pl.multiple_of(blk * SUBLANES, SUBLANES), SUBLANES)
        a = a_ref[d, rows, :]
        h = b_ref[d, rows, :] + a * h_prev
        if h_ref is not None:
            h_ref[d, rows, :] = h
        return (jnp.broadcast_to(h[edge:edge + 1, :], (SUBLANES, n)),
                a_prod * a[edge:edge + 1, :])

    init = (jnp.broadcast_to(h0, (SUBLANES, n)), jnp.ones((1, n), F32))
    h_last, a_prod = lax.fori_loop(0, nblk, body, init, unroll=4)
    return h_last[0:1, :], a_prod


def _tile_position(tiles_per_seq):
    i = pl.program_id(0)
    k = i % tiles_per_seq
    return k, k == 0, k == tiles_per_seq - 1


def _pre_kernel(x_ref, xp_ref, xn_ref, mod_ref, gpre_ref, wxr_ref, cw_ref, cb_ref, wg_ref, bg_ref,
                lam_ref, sum_ref, hext_ref, xr_ref, xc_ref, a_ref, b_ref, *, tiles_per_seq):
    tm, dm = x_ref.shape
    _, first, last = _tile_position(tiles_per_seq)
    mod = mod_ref[0]
    sh1, sc1 = mod[:, 0:dm], mod[:, dm:2 * dm]
    _modulated_norm_ext(x_ref, xp_ref, xn_ref, gpre_ref, sh1, sc1, first, last, hext_ref)
    xr_ref[...] = _dot(hext_ref[...], wxr_ref[...])
    xc_ref[...] = _conv(xr_ref, cw_ref, cb_ref, tm)
    coef = _decay_log2_coeff(lam_ref)
    _gates_to_scan_inputs(xc_ref, wg_ref, bg_ref, coef, a_ref, b_ref)
    zero = jnp.zeros((1, dm), F32)
    for d in range(2):
        h_end, a_prod = _scan_blocks(a_ref, b_ref, d, zero, None)
        sum_ref[0, 0, 2 * d:2 * d + 1, :] = a_prod
        sum_ref[0, 0, 2 * d + 1:2 * d + 2, :] = h_end


def _mix_kernel(x_ref, xp_ref, xn_ref, mod_ref, gpre_ref, gpost_ref, gsgu_ref, win_ref, wsp_ref,
                bsp_ref, cw_ref, cb_ref, wg_ref, bg_ref, lam_ref, wbg_ref, wbr_ref, wout_ref,
                sum_ref, h0_ref, o_ref, st_ref,
                hext_ref, gu_ref, vn_ref, yg_ref, xr_ref, xc_ref, a_ref, b_ref, h_ref, yr_ref,
                *, tiles_per_seq):
    tm, dm = x_ref.shape
    k, first, last = _tile_position(tiles_per_seq)
    mod = mod_ref[0]
    sh1, sc1, g1 = mod[:, 0:dm], mod[:, dm:2 * dm], mod[:, 2 * dm:3 * dm]
    _modulated_norm_ext(x_ref, xp_ref, xn_ref, gpre_ref, sh1, sc1, first, last, hext_ref)
    rows = slice(HALO, HALO + tm)

    def col(j):
        return slice(j * dm, (j + 1) * dm)

    gu_ref[...] = _gelu(_dot(hext_ref[rows, :], win_ref[:, col(0)]))
    v = _gelu(_dot(hext_ref[rows, :], win_ref[:, col(1)]))
    vn_ref[...] = (_rms_scale(v) * gsgu_ref[...]).astype(BF16)
    for n in range(tm // CHUNK):
        for g in range(G_HEADS):
            rs = slice(n * CHUNK, (n + 1) * CHUNK)
            cs = slice(g * CHUNK, (g + 1) * CHUNK)
            s = _dot(wsp_ref[g], vn_ref[rs, cs]) + bsp_ref[g]
            yg_ref[rs, cs] = (gu_ref[rs, cs] * s).astype(BF16)

    xr_ref[...] = _dot(hext_ref[...], win_ref[:, col(2)])
    xc_ref[...] = _conv(xr_ref, cw_ref, cb_ref, tm)
    coef = _decay_log2_coeff(lam_ref)
    _gates_to_scan_inputs(xc_ref, wg_ref, bg_ref, coef, a_ref, b_ref)

    hf0 = h0_ref[0, 0:1, :]
    hb0 = h0_ref[0, 1:2, :]
    for j in range(tiles_per_seq - 1):
        hf0 = jnp.where(j < k, sum_ref[0, j, 0:1, :] * hf0 + sum_ref[0, j, 1:2, :], hf0)
    for j in range(tiles_per_seq - 1, 0, -1):
        hb0 = jnp.where(j > k, sum_ref[0, j, 2:3, :] * hb0 + sum_ref[0, j, 3:4, :], hb0)
    hf_end, _ = _scan_blocks(a_ref, b_ref, 0, hf0, h_ref)
    hb_end, _ = _scan_blocks(a_ref, b_ref, 1, hb0, h_ref)
    st_ref[0, 0:1, :] = hf_end
    st_ref[0, 1:2, :] = hb_end
    gr = _gelu(_dot(hext_ref[rows, :], win_ref[:, col(3)]))
    yr_ref[...] = ((h_ref[0] + h_ref[1]) * gr).astype(BF16)

    ga = _sigmoid(_dot(hext_ref[rows, :], win_ref[:, col(4)]))
    merged = ga * _dot(yg_ref[...], wbg_ref[...])
    gb = _sigmoid(_dot(hext_ref[rows, :], win_ref[:, col(5)]))
    merged = merged + gb * _dot(yr_ref[...], wbr_ref[...])
    o = _dot(merged.astype(BF16), wout_ref[...])
    o_ref[...] = x_ref[...] + g1 * (_rms_scale(o) * gpost_ref[...])


def _mlp_kernel(x_ref, mod_ref, gpre_ref, gpost_ref, w1_ref, w2_ref, o_ref):
    dm = x_ref.shape[1]
    mod = mod_ref[0]
    sh2, sc2, g2 = mod[:, 3 * dm:4 * dm], mod[:, 4 * dm:5 * dm], mod[:, 5 * dm:6 * dm]
    x = x_ref[...]
    h = (_rms_scale(x) * (gpre_ref[...] * (1.0 + sc2)) + sh2).astype(BF16)
    f1 = jnp.maximum(_dot(h, w1_ref[...]), 0.0)
    f = _dot((f1 * f1).astype(BF16), w2_ref[...])
    o_ref[...] = x + g2 * (_rms_scale(f) * gpost_ref[...])


def _const_spec(shape):
    nd = len(shape)
    return pl.BlockSpec(shape, lambda i: (0,) * nd, pipeline_mode=pl.Buffered(1))


def _halo_specs(tm, dm, n_rows):
    per_tile = tm // HALO
    n_blocks = n_rows // HALO
    prev = pl.BlockSpec((HALO, dm), lambda i: (jnp.maximum(i * per_tile - 1, 0), 0))
    nxt = pl.BlockSpec((HALO, dm), lambda i: (jnp.minimum((i + 1) * per_tile, n_blocks - 1), 0))
    return prev, nxt


def _mod_spec(n_mod_cols, tiles_per_seq, mod_row0, per_seq_mod):
    if per_seq_mod:
        return pl.BlockSpec((1, 1, n_mod_cols), lambda i: (mod_row0 + i // tiles_per_seq, 0, 0))
    return pl.BlockSpec((1, 1, n_mod_cols), lambda i: (mod_row0, 0, 0))


def _pre(x2d, mod3, p, *, tm, tiles_per_seq, mod_row0):
    n_rows, dm = x2d.shape
    n_tiles = n_rows // tm
    n_seq = n_tiles // tiles_per_seq
    prev, nxt = _halo_specs(tm, dm, n_rows)
    ext = tm + 2 * HALO
    return pl.pallas_call(
        functools.partial(_pre_kernel, tiles_per_seq=tiles_per_seq),
        grid=(n_tiles,),
        in_specs=[
            pl.BlockSpec((tm, dm), lambda i: (i, 0)), prev, nxt,
            _mod_spec(mod3.shape[2], tiles_per_seq, mod_row0, True),
            _const_spec((1, dm)),
            pl.BlockSpec((dm, dm), lambda i: (0, 2), pipeline_mode=pl.Buffered(1)),
            _const_spec(p["conv_w"].shape), _const_spec((1, dm)),
            _const_spec(p["wg"].shape), _const_spec(p["bg"].shape), _const_spec(p["lam"].shape),
        ],
        out_specs=pl.BlockSpec((1, 1, 4, dm), lambda i: (i // tiles_per_seq, i % tiles_per_seq, 0, 0)),
        out_shape=jax.ShapeDtypeStruct((n_seq, tiles_per_seq, 4, dm), F32),
        scratch_shapes=[
            pltpu.VMEM((ext, dm), BF16), pltpu.VMEM((ext, dm), F32), pltpu.VMEM((tm, dm), F32),
            pltpu.VMEM((2, tm, dm), F32), pltpu.VMEM((2, tm, dm), F32),
        ],
        compiler_params=pltpu.CompilerParams(
            dimension_semantics=("arbitrary",), vmem_limit_bytes=VMEM_LIMIT_BYTES),
        name="lru_summaries",
    )(x2d, x2d, x2d, mod3, p["g_pre_mix"], p["w_in"], p["conv_w"], p["conv_b"], p["wg"], p["bg"],
      p["lam"])


def _mix(x2d, mod3, summaries, h0, p, *, tm, tiles_per_seq, mod_row0, per_seq_mod):
    n_rows, dm = x2d.shape
    n_tiles = n_rows // tm
    n_seq = n_tiles // tiles_per_seq
    prev, nxt = _halo_specs(tm, dm, n_rows)
    ext = tm + 2 * HALO
    return pl.pallas_call(
        functools.partial(_mix_kernel, tiles_per_seq=tiles_per_seq),
        grid=(n_tiles,),
        in_specs=[
            pl.BlockSpec((tm, dm), lambda i: (i, 0)), prev, nxt,
            _mod_spec(mod3.shape[2], tiles_per_seq, mod_row0, per_seq_mod),
            _const_spec((1, dm)), _const_spec((1, dm)), _const_spec((1, dm)),
            _const_spec(p["w_in"].shape), _const_spec(p["w_sp"].shape), _const_spec(p["b_sp"].shape),
            _const_spec(p["conv_w"].shape), _const_spec((1, dm)),
            _const_spec(p["wg"].shape), _const_spec(p["bg"].shape), _const_spec(p["lam"].shape),
            _const_spec((dm, dm)), _const_spec((dm, dm)), _const_spec((dm, dm)),
            pl.BlockSpec((1, tiles_per_seq, 4, dm), lambda i: (i // tiles_per_seq, 0, 0, 0)),
            pl.BlockSpec((1, 2, dm), lambda i: (i // tiles_per_seq, 0, 0)),
        ],
        out_specs=[
            pl.BlockSpec((tm, dm), lambda i: (i, 0)),
            pl.BlockSpec((1, 2, dm), lambda i: (i, 0, 0)),
        ],
        out_shape=[
            jax.ShapeDtypeStruct((n_rows, dm), F32),
            jax.ShapeDtypeStruct((n_tiles, 2, dm), F32),
        ],
        scratch_shapes=[
            pltpu.VMEM((ext, dm), BF16),
            pltpu.VMEM((tm, dm), F32),
            pltpu.VMEM((tm, dm), BF16),
            pltpu.VMEM((tm, dm), BF16),
            pltpu.VMEM((ext, dm), F32),
            pltpu.VMEM((tm, dm), F32),
            pltpu.VMEM((2, tm, dm), F32),
            pltpu.VMEM((2, tm, dm), F32),
            pltpu.VMEM((2, tm, dm), F32),
            pltpu.VMEM((tm, dm), BF16),
        ],
        compiler_params=pltpu.CompilerParams(
            dimension_semantics=("arbitrary",), vmem_limit_bytes=VMEM_LIMIT_BYTES),
        name="token_mix",
    )(x2d, x2d, x2d, mod3, p["g_pre_mix"], p["g_post_mix"], p["g_sgu"], p["w_in"], p["w_sp"],
      p["b_sp"], p["conv_w"], p["conv_b"], p["wg"], p["bg"], p["lam"], p["w_br_g"], p["w_br_r"],
      p["w_out"], summaries, h0)


def _mlp(x2d, mod3, p, *, tm, tiles_per_seq, mod_row0, per_seq_mod):
    n_rows, dm = x2d.shape
    return pl.pallas_call(
        _mlp_kernel,
        grid=(n_rows // tm,),
        in_specs=[
            pl.BlockSpec((tm, dm), lambda i: (i, 0)),
            _mod_spec(mod3.shape[2], tiles_per_seq, mod_row0, per_seq_mod),
            _const_spec((1, dm)), _const_spec((1, dm)),
            _const_spec(p["w_ff1"].shape), _const_spec(p["w_ff2"].shape),
        ],
        out_specs=pl.BlockSpec((tm, dm), lambda i: (i, 0)),
        out_shape=jax.ShapeDtypeStruct((n_rows, dm), F32),
        compiler_params=pltpu.CompilerParams(
            dimension_semantics=("arbitrary",), vmem_limit_bytes=VMEM_LIMIT_BYTES),
        name="relu2_mlp",
    )(x2d, mod3, p["g_pre_mlp"], p["g_post_mlp"], p["w_ff1"], p["w_ff2"])


def _gate_weights(w_ra, w_ri, b_ra, b_ri):
    n_groups = LRU_HEADS // HEADS_PER_GROUP
    w_all = jnp.stack([w_ra[0], w_ra[1], w_ri[0], w_ri[1]])
    w_all = w_all.reshape(4, n_groups, HEADS_PER_GROUP, LRU_DIM, LRU_DIM)
    eye = jnp.eye(HEADS_PER_GROUP, dtype=w_all.dtype)
    bd = w_all[:, :, :, :, None, :] * eye[None, None, :, None, :, None]
    wg = bd.transpose(1, 2, 3, 0, 4, 5).reshape(n_groups, GATE_GROUP, 4 * GATE_GROUP)
    b_all = jnp.stack([b_ra[0], b_ra[1], b_ri[0], b_ri[1]])
    bg = b_all.reshape(4, n_groups, GATE_GROUP).transpose(1, 0, 2).reshape(n_groups, 1, 4 * GATE_GROUP)
    return wg.astype(BF16), bg


def kernel(x_prompt, x_sample, state_lru, c, c_ctx, w_ada, b_ada, g_pre_mix, g_post_mix, g_pre_mlp, g_post_mlp, w_in, g_sgu, w_sp, b_sp, conv_w, conv_b, w_ra, b_ra, w_ri, b_ri, lam, w_br_g, w_br_r, w_out, w_ff1, w_ff2):
    batch, seq, dm = x_prompt.shape
    dec_batch, dec_seq, _ = x_sample.shape
    depth = w_ada.shape[0]
    tm = 256
    assert seq == tm and dec_seq % tm == 0 and SUBLANES - 1 >= dec_batch

    c_rows = jnp.concatenate(
        [c_ctx[None, :], c, jnp.zeros((SUBLANES - 1 - dec_batch, dm), F32)], axis=0)
    yp = x_prompt.reshape(batch * seq, dm)
    ys = x_sample.reshape(dec_batch * dec_seq, dm)
    zeros_sum = jnp.zeros((batch, 1, 4, dm), F32)
    zeros_h0 = jnp.zeros((batch, 2, dm), F32)
    states = []
    for l in range(depth):
        wg, bg = _gate_weights(w_ra[l], w_ri[l], b_ra[l], b_ri[l])
        p = {
            "g_pre_mix": g_pre_mix[l][None, :], "g_post_mix": g_post_mix[l][None, :],
            "g_pre_mlp": g_pre_mlp[l][None, :], "g_post_mlp": g_post_mlp[l][None, :],
            "g_sgu": g_sgu[l][None, :],
            "w_in": w_in[l].astype(BF16), "w_sp": w_sp[l].astype(BF16),
            "b_sp": jnp.broadcast_to(b_sp[l][:, :, None], (G_HEADS, CHUNK, CHUNK)),
            "conv_w": conv_w[l], "conv_b": conv_b[l][None, :],
            "wg": wg, "bg": bg, "lam": lam[l],
            "w_br_g": w_br_g[l].astype(BF16), "w_br_r": w_br_r[l].astype(BF16),
            "w_out": w_out[l].astype(BF16),
            "w_ff1": w_ff1[l].astype(BF16), "w_ff2": w_ff2[l].astype(BF16),
        }
        mod = _ada(c_rows, w_ada[l], b_ada[l][None, :])
        mod3 = mod.reshape(SUBLANES, 1, N_MOD * dm)

        yp, st = _mix(yp, mod3, zeros_sum, zeros_h0, p, tm=tm, tiles_per_seq=1, mod_row0=0,
                      per_seq_mod=False)
        yp = _mlp(yp, mod3, p, tm=tm, tiles_per_seq=1, mod_row0=0, per_seq_mod=False)
        states.append(st)

        tps = dec_seq // tm
        summaries = _pre(ys, mod3, p, tm=tm, tiles_per_seq=tps, mod_row0=1)
        h0 = state_lru[:, l].astype(F32)
        ys, _ = _mix(ys, mod3, summaries, h0, p, tm=tm, tiles_per_seq=tps, mod_row0=1,
                     per_seq_mod=True)
        ys = _mlp(ys, mod3, p, tm=tm, tiles_per_seq=tps, mod_row0=1, per_seq_mod=True)

    new_state = jnp.stack(states, axis=1).astype(x_prompt.dtype)
    return (yp.reshape(batch, seq, dm), ys.reshape(dec_batch, dec_seq, dm), new_state)
```

```python
import functools

import jax
import jax.numpy as jnp
from jax import lax
from jax.experimental import pallas as pl
from jax.experimental.pallas import tpu as pltpu

F32 = jnp.float32
BF16 = jnp.bfloat16

SUBLANES = 8
LANES = 128
BF16_ROWS = 16
VMEM_LIMIT_BYTES = 56 * 1024 * 1024

CHUNK = 128
G_HEADS = 8
LRU_HEADS = 16
LRU_DIM = 64
CONV_W = 4
CONV_PAD_L = 2
LRU_C = 8.0
N_MOD = 6
EPS = 1e-6
HALO = BF16_ROWS
HEADS_PER_GROUP = 4
GATE_GROUP = HEADS_PER_GROUP * LRU_DIM


def _sigmoid(x):
    return 0.5 * jnp.tanh(0.5 * x) + 0.5


def _gelu(x):
    return 0.5 * x * (1.0 + jnp.tanh(0.7978845608028654 * (x + 0.044715 * (x * x * x))))


def _rms_scale(x):
    return x * lax.rsqrt(jnp.mean(x * x, axis=-1, keepdims=True) + EPS)


def _dot(a, b):
    return jnp.dot(a, b, preferred_element_type=F32)


def _ada_kernel(c_ref, w_ref, b_ref, o_ref):
    c = c_ref[...]
    s = c * _sigmoid(c)
    s_hi = s.astype(BF16)
    s_lo = (s - s_hi.astype(F32)).astype(BF16)
    w = w_ref[...]
    w_hi = w.astype(BF16)
    w_lo = (w - w_hi.astype(F32)).astype(BF16)
    o_ref[...] = _dot(s_hi, w_hi) + _dot(s_lo, w_hi) + _dot(s_hi, w_lo) + b_ref[...]


def _ada(c_rows, w_ada, b_ada):
    d, n = w_ada.shape
    bn = 1536
    return pl.pallas_call(
        _ada_kernel,
        grid=(n // bn,),
        in_specs=[
            pl.BlockSpec((SUBLANES, d), lambda j: (0, 0)),
            pl.BlockSpec((d, bn), lambda j: (0, j)),
            pl.BlockSpec((1, bn), lambda j: (0, j)),
        ],
        out_specs=pl.BlockSpec((SUBLANES, bn), lambda j: (0, j)),
        out_shape=jax.ShapeDtypeStruct((SUBLANES, n), F32),
        compiler_params=pltpu.CompilerParams(
            dimension_semantics=("arbitrary",), vmem_limit_bytes=VMEM_LIMIT_BYTES),
        name="ada_mod",
    )(c_rows, w_ada, b_ada)


def _modulated_norm_ext(x_ref, xp_ref, xn_ref, g_ref, sh, sc, first, last, hext_ref):
    tm = x_ref.shape[0]
    scale = g_ref[...] * (1.0 + sc)

    def norm(x):
        return _rms_scale(x) * scale + sh

    hp = jnp.where(first, 0.0, norm(xp_ref[...]))
    hn = jnp.where(last, 0.0, norm(xn_ref[...]))
    hext_ref[0:HALO, :] = hp.astype(BF16)
    hext_ref[HALO:HALO + tm, :] = norm(x_ref[...]).astype(BF16)
    hext_ref[HALO + tm:HALO + tm + HALO, :] = hn.astype(BF16)


def _scan_geometry(tm):
    seg = tm // SUBLANES
    assert seg % SUBLANES == 0
    pitch = seg if (seg // SUBLANES) % 2 else seg + SUBLANES
    return seg, pitch


def _store_time_major(ref, idx, val):
    seg = val.shape[0] // SUBLANES
    for s in range(SUBLANES):
        ref[idx + (pl.ds(s, seg, stride=SUBLANES), slice(None))] = val[s * seg:(s + 1) * seg, :]


def _load_subseq(ref, idx, tm):
    seg, pitch = _scan_geometry(tm)
    return jnp.concatenate(
        [ref[idx + (pl.ds(s * pitch, seg), slice(None))] for s in range(SUBLANES)], axis=0)


def _project_and_conv(hext_ref, wxr, cw_ref, cb_ref, xr_ref, xc_ref):
    tm = xc_ref.shape[0]
    xr = _dot(hext_ref[...], wxr)
    for lb in range(xr_ref.shape[0]):
        xr_ref[lb] = xr[:, lb * LANES:(lb + 1) * LANES]
    for lb in range(xr_ref.shape[0]):
        cols = slice(lb * LANES, (lb + 1) * LANES)
        acc = cb_ref[:, cols]
        for k in range(CONV_W):
            off = HALO + k - CONV_PAD_L
            acc = acc + xr_ref[lb, pl.ds(off, tm), :] * cw_ref[k:k + 1, cols]
        xc_ref[:, cols] = acc


def _decay_coeff(lam_ref):
    nl = -lam_ref[...]
    softplus = jnp.maximum(nl, 0.0) + jnp.log1p(jnp.exp(-jnp.abs(nl)))
    return -LRU_C * softplus


def _sublane_prefix(a, b, reverse):
    row = lax.broadcasted_iota(jnp.int32, a.shape, 0)
    for d in (1, 2, 4):
        if reverse:
            shift, valid = SUBLANES - d, row < SUBLANES - d
        else:
            shift, valid = d, row >= d
        a_sh = jnp.where(valid, pltpu.roll(a, shift, 0), 1.0)
        b_sh = jnp.where(valid, pltpu.roll(b, shift, 0), 0.0)
        b = a * b_sh + b
        a = a * a_sh
    return a, b


def _gates_to_scan_inputs(xc_ref, wg_ref, bg_ref, coef, a_ref, b_ref):
    n_groups = wg_ref.shape[0]
    lb_per_group = GATE_GROUP // LANES
    for g in range(n_groups):
        cols = slice(g * GATE_GROUP, (g + 1) * GATE_GROUP)
        xc = xc_ref[:, cols]
        gate = _dot(xc.astype(BF16), wg_ref[g]) + bg_ref[g]
        for d in range(2):
            r = _sigmoid(gate[:, d * GATE_GROUP:(d + 1) * GATE_GROUP])
            ig = _sigmoid(gate[:, (2 + d) * GATE_GROUP:(3 + d) * GATE_GROUP])
            log_a = r * coef[d:d + 1, cols]
            a = jnp.exp(log_a)
            t = jnp.tanh(log_a)
            m = jnp.maximum((-2.0 * t) / (1.0 - t), 1e-12)
            bx = (m * lax.rsqrt(m)) * ig * xc
            for j in range(lb_per_group):
                lb = g * lb_per_group + j
                _store_time_major(a_ref, (d, lb), a[:, j * LANES:(j + 1) * LANES])
                _store_time_major(b_ref, (d, lb), bx[:, j * LANES:(j + 1) * LANES])


def _scan_tile(a_ref, b_ref, d, h0, h_ref, tm):
    seg, pitch = _scan_geometry(tm)
    n_lb = a_ref.shape[1]
    reverse = d == 1
    steps = [seg - 1 - i if reverse else i for i in range(seg)]

    ends = []
    for lb in range(n_lb):
        h = jnp.zeros((SUBLANES, LANES), F32)
        p = jnp.ones((SUBLANES, LANES), F32)
        for j in steps:
            a = a_ref[d, lb, pl.ds(j * SUBLANES, SUBLANES), :]
            h = a * h + b_ref[d, lb, pl.ds(j * SUBLANES, SUBLANES), :]
            p = a * p
        ends.append((h, p))

    row = lax.broadcasted_iota(jnp.int32, (SUBLANES, LANES), 0)
    first_row = SUBLANES - 1 if reverse else 0
    last_row = 0 if reverse else SUBLANES - 1
    shift_in = SUBLANES - 1 if reverse else 1
    inits, tile_end, tile_prod = [], [], []
    for lb in range(n_lb):
        h_end, p_end = ends[lb]
        pp, hh = _sublane_prefix(p_end, h_end, reverse)
        c0 = jnp.broadcast_to(h0[:, lb * LANES:(lb + 1) * LANES], (SUBLANES, LANES))
        full = pp * c0 + hh
        inits.append(jnp.where(row == first_row, c0, pltpu.roll(full, shift_in, 0)))
        tile_end.append(full[last_row:last_row + 1, :])
        tile_prod.append(pp[last_row:last_row + 1, :])

    if h_ref is not None:
        for lb in range(n_lb):
            h = inits[lb]
            for j in steps:
                rows = pl.ds(j * SUBLANES, SUBLANES)
                h = a_ref[d, lb, rows, :] * h + b_ref[d, lb, rows, :]
                h_ref[d, lb, pl.ds(j, SUBLANES, stride=pitch), :] = h
    return jnp.concatenate(tile_end, axis=1), jnp.concatenate(tile_prod, axis=1)


def _tile_position(tiles_per_seq):
    i = pl.program_id(0)
    k = i % tiles_per_seq
    return k, k == 0, k == tiles_per_seq - 1


def _pre_kernel(x_ref, xp_ref, xn_ref, mod_ref, gpre_ref, wxr_ref, cw_ref, cb_ref, wg_ref, bg_ref,
                lam_ref, sum_ref, hext_ref, xr_ref, xc_ref, a_ref, b_ref, *, tiles_per_seq):
    tm, dm = x_ref.shape
    _, first, last = _tile_position(tiles_per_seq)
    mod = mod_ref[0]
    sh1, sc1 = mod[:, 0:dm], mod[:, dm:2 * dm]
    _modulated_norm_ext(x_ref, xp_ref, xn_ref, gpre_ref, sh1, sc1, first, last, hext_ref)
    _project_and_conv(hext_ref, wxr_ref[...], cw_ref, cb_ref, xr_ref, xc_ref)
    _gates_to_scan_inputs(xc_ref, wg_ref, bg_ref, _decay_coeff(lam_ref), a_ref, b_ref)
    zero = jnp.zeros((1, dm), F32)
    for d in range(2):
        h_end, a_prod = _scan_tile(a_ref, b_ref, d, zero, None, tm)
        sum_ref[0, 0, 2 * d:2 * d + 1, :] = a_prod
        sum_ref[0, 0, 2 * d + 1:2 * d + 2, :] = h_end


def _mix_kernel(x_ref, xp_ref, xn_ref, mod_ref, gpre_ref, gpost_ref, gsgu_ref, win_ref, wsp_ref,
                bsp_ref, cw_ref, cb_ref, wg_ref, bg_ref, lam_ref, wbg_ref, wbr_ref, wout_ref,
                sum_ref, h0_ref, o_ref, st_ref,
                hext_ref, gu_ref, vn_ref, yg_ref, xr_ref, xc_ref, a_ref, b_ref, h_ref, yr_ref,
                *, tiles_per_seq):
    tm, dm = x_ref.shape
    k, first, last = _tile_position(tiles_per_seq)
    mod = mod_ref[0]
    sh1, sc1, g1 = mod[:, 0:dm], mod[:, dm:2 * dm], mod[:, 2 * dm:3 * dm]
    _modulated_norm_ext(x_ref, xp_ref, xn_ref, gpre_ref, sh1, sc1, first, last, hext_ref)
    rows = slice(HALO, HALO + tm)

    def col(j):
        return slice(j * dm, (j + 1) * dm)

    gu_ref[...] = _gelu(_dot(hext_ref[rows, :], win_ref[:, col(0)]))
    v = _gelu(_dot(hext_ref[rows, :], win_ref[:, col(1)]))
    vn_ref[...] = (_rms_scale(v) * gsgu_ref[...]).astype(BF16)
    for n in range(tm // CHUNK):
        for g in range(G_HEADS):
            rs = slice(n * CHUNK, (n + 1) * CHUNK)
            cs = slice(g * CHUNK, (g + 1) * CHUNK)
            s = _dot(wsp_ref[g], vn_ref[rs, cs]) + bsp_ref[g]
            yg_ref[rs, cs] = (gu_ref[rs, cs] * s).astype(BF16)

    _project_and_conv(hext_ref, win_ref[:, col(2)], cw_ref, cb_ref, xr_ref, xc_ref)
    _gates_to_scan_inputs(xc_ref, wg_ref, bg_ref, _decay_coeff(lam_ref), a_ref, b_ref)

    hf0 = h0_ref[0, 0:1, :]
    hb0 = h0_ref[0, 1:2, :]
    for j in range(tiles_per_seq - 1):
        hf0 = jnp.where(j < k, sum_ref[0, j, 0:1, :] * hf0 + sum_ref[0, j, 1:2, :], hf0)
    for j in range(tiles_per_seq - 1, 0, -1):
        hb0 = jnp.where(j > k, sum_ref[0, j, 2:3, :] * hb0 + sum_ref[0, j, 3:4, :], hb0)
    hf_end, _ = _scan_tile(a_ref, b_ref, 0, hf0, h_ref, tm)
    hb_end, _ = _scan_tile(a_ref, b_ref, 1, hb0, h_ref, tm)
    st_ref[0, 0:1, :] = hf_end
    st_ref[0, 1:2, :] = hb_end
    gr = _gelu(_dot(hext_ref[rows, :], win_ref[:, col(3)]))
    for lb in range(h_ref.shape[1]):
        cols = slice(lb * LANES, (lb + 1) * LANES)
        h_sum = _load_subseq(h_ref, (0, lb), tm) + _load_subseq(h_ref, (1, lb), tm)
        yr_ref[:, cols] = (h_sum * gr[:, cols]).astype(BF16)

    ga = _sigmoid(_dot(hext_ref[rows, :], win_ref[:, col(4)]))
    merged = ga * _dot(yg_ref[...], wbg_ref[...])
    gb = _sigmoid(_dot(hext_ref[rows, :], win_ref[:, col(5)]))
    merged = merged + gb * _dot(yr_ref[...], wbr_ref[...])
    o = _dot(merged.astype(BF16), wout_ref[...])
    o_ref[...] = x_ref[...] + g1 * (_rms_scale(o) * gpost_ref[...])


def _mlp_kernel(x_ref, mod_ref, gpre_ref, gpost_ref, w1_ref, w2_ref, o_ref):
    dm = x_ref.shape[1]
    mod = mod_ref[0]
    sh2, sc2, g2 = mod[:, 3 * dm:4 * dm], mod[:, 4 * dm:5 * dm], mod[:, 5 * dm:6 * dm]
    x = x_ref[...]
    h = (_rms_scale(x) * (gpre_ref[...] * (1.0 + sc2)) + sh2).astype(BF16)
    f1 = jnp.maximum(_dot(h, w1_ref[...]), 0.0)
    f = _dot((f1 * f1).astype(BF16), w2_ref[...])
    o_ref[...] = x + g2 * (_rms_scale(f) * gpost_ref[...])


def _const_spec(shape):
    nd = len(shape)
    return pl.BlockSpec(shape, lambda i: (0,) * nd, pipeline_mode=pl.Buffered(1))


def _halo_specs(tm, dm, n_rows):
    per_tile = tm // HALO
    n_blocks = n_rows // HALO
    prev = pl.BlockSpec((HALO, dm), lambda i: (jnp.maximum(i * per_tile - 1, 0), 0))
    nxt = pl.BlockSpec((HALO, dm), lambda i: (jnp.minimum((i + 1) * per_tile, n_blocks - 1), 0))
    return prev, nxt


def _mod_spec(n_mod_cols, tiles_per_seq, mod_row0, per_seq_mod):
    if per_seq_mod:
        return pl.BlockSpec((1, 1, n_mod_cols), lambda i: (mod_row0 + i // tiles_per_seq, 0, 0))
    return pl.BlockSpec((1, 1, n_mod_cols), lambda i: (mod_row0, 0, 0))


def _scan_input_scratch(tm, dm):
    return pltpu.VMEM((2, dm // LANES, tm, LANES), F32)


def _scan_state_scratch(tm, dm):
    _, pitch = _scan_geometry(tm)
    return pltpu.VMEM((2, dm // LANES, SUBLANES * pitch, LANES), F32)


def _pre(x2d, mod3, p, *, tm, tiles_per_seq, mod_row0):
    n_rows, dm = x2d.shape
    n_tiles = n_rows // tm
    n_seq = n_tiles // tiles_per_seq
    prev, nxt = _halo_specs(tm, dm, n_rows)
    ext = tm + 2 * HALO
    return pl.pallas_call(
        functools.partial(_pre_kernel, tiles_per_seq=tiles_per_seq),
        grid=(n_tiles,),
        in_specs=[
            pl.BlockSpec((tm, dm), lambda i: (i, 0)), prev, nxt,
            _mod_spec(mod3.shape[2], tiles_per_seq, mod_row0, True),
            _const_spec((1, dm)),
            pl.BlockSpec((dm, dm), lambda i: (0, 2), pipeline_mode=pl.Buffered(1)),
            _const_spec(p["conv_w"].shape), _const_spec((1, dm)),
            _const_spec(p["wg"].shape), _const_spec(p["bg"].shape), _const_spec(p["lam"].shape),
        ],
        out_specs=pl.BlockSpec((1, 1, 4, dm), lambda i: (i // tiles_per_seq, i % tiles_per_seq, 0, 0)),
        out_shape=jax.ShapeDtypeStruct((n_seq, tiles_per_seq, 4, dm), F32),
        scratch_shapes=[
            pltpu.VMEM((ext, dm), BF16),
            pltpu.VMEM((dm // LANES, ext, LANES), F32),
            pltpu.VMEM((tm, dm), F32),
            _scan_input_scratch(tm, dm), _scan_input_scratch(tm, dm),
        ],
        compiler_params=pltpu.CompilerParams(
            dimension_semantics=("arbitrary",), vmem_limit_bytes=VMEM_LIMIT_BYTES),
        name="lru_summaries",
    )(x2d, x2d, x2d, mod3, p["g_pre_mix"], p["w_in"], p["conv_w"], p["conv_b"], p["wg"], p["bg"],
      p["lam"])


def _mix(x2d, mod3, summaries, h0, p, *, tm, tiles_per_seq, mod_row0, per_seq_mod):
    n_rows, dm = x2d.shape
    n_tiles = n_rows // tm
    prev, nxt = _halo_specs(tm, dm, n_rows)
    ext = tm + 2 * HALO
    return pl.pallas_call(
        functools.partial(_mix_kernel, tiles_per_seq=tiles_per_seq),
        grid=(n_tiles,),
        in_specs=[
            pl.BlockSpec((tm, dm), lambda i: (i, 0)), prev, nxt,
            _mod_spec(mod3.shape[2], tiles_per_seq, mod_row0, per_seq_mod),
            _const_spec((1, dm)), _const_spec((1, dm)), _const_spec((1, dm)),
            _const_spec(p["w_in"].shape), _const_spec(p["w_sp"].shape), _const_spec(p["b_sp"].shape),
            _const_spec(p["conv_w"].shape), _const_spec((1, dm)),
            _const_spec(p["wg"].shape), _const_spec(p["bg"].shape), _const_spec(p["lam"].shape),
            _const_spec((dm, dm)), _const_spec((dm, dm)), _const_spec((dm, dm)),
            pl.BlockSpec((1, tiles_per_seq, 4, dm), lambda i: (i // tiles_per_seq, 0, 0, 0)),
            pl.BlockSpec((1, 2, dm), lambda i: (i // tiles_per_seq, 0, 0)),
        ],
        out_specs=[
            pl.BlockSpec((tm, dm), lambda i: (i, 0)),
            pl.BlockSpec((1, 2, dm), lambda i: (i, 0, 0)),
        ],
        out_shape=[
            jax.ShapeDtypeStruct((n_rows, dm), F32),
            jax.ShapeDtypeStruct((n_tiles, 2, dm), F32),
        ],
        scratch_shapes=[
            pltpu.VMEM((ext, dm), BF16),
            pltpu.VMEM((tm, dm), F32),
            pltpu.VMEM((tm, dm), BF16),
            pltpu.VMEM((tm, dm), BF16),
            pltpu.VMEM((dm // LANES, ext, LANES), F32),
            pltpu.VMEM((tm, dm), F32),
            _scan_input_scratch(tm, dm),
            _scan_input_scratch(tm, dm),
            _scan_state_scratch(tm, dm),
            pltpu.VMEM((tm, dm), BF16),
        ],
        compiler_params=pltpu.CompilerParams(
            dimension_semantics=("arbitrary",), vmem_limit_bytes=VMEM_LIMIT_BYTES),
        name="token_mix",
    )(x2d, x2d, x2d, mod3, p["g_pre_mix"], p["g_post_mix"], p["g_sgu"], p["w_in"], p["w_sp"],
      p["b_sp"], p["conv_w"], p["conv_b"], p["wg"], p["bg"], p["lam"], p["w_br_g"], p["w_br_r"],
      p["w_out"], summaries, h0)


def _mlp(x2d, mod3, p, *, tm, tiles_per_seq, mod_row0, per_seq_mod):
    n_rows, dm = x2d.shape
    return pl.pallas_call(
        _mlp_kernel,
        grid=(n_rows // tm,),
        in_specs=[
            pl.BlockSpec((tm, dm), lambda i: (i, 0)),
            _mod_spec(mod3.shape[2], tiles_per_seq, mod_row0, per_seq_mod),
            _const_spec((1, dm)), _const_spec((1, dm)),
            _const_spec(p["w_ff1"].shape), _const_spec(p["w_ff2"].shape),
        ],
        out_specs=pl.BlockSpec((tm, dm), lambda i: (i, 0)),
        out_shape=jax.ShapeDtypeStruct((n_rows, dm), F32),
        compiler_params=pltpu.CompilerParams(
            dimension_semantics=("arbitrary",), vmem_limit_bytes=VMEM_LIMIT_BYTES),
        name="relu2_mlp",
    )(x2d, mod3, p["g_pre_mlp"], p["g_post_mlp"], p["w_ff1"], p["w_ff2"])


def _gate_weights(w_ra, w_ri, b_ra, b_ri):
    n_groups = LRU_HEADS // HEADS_PER_GROUP
    w_all = jnp.stack([w_ra[0], w_ra[1], w_ri[0], w_ri[1]])
    w_all = w_all.reshape(4, n_groups, HEADS_PER_GROUP, LRU_DIM, LRU_DIM)
    eye = jnp.eye(HEADS_PER_GROUP, dtype=w_all.dtype)
    bd = w_all[:, :, :, :, None, :] * eye[None, None, :, None, :, None]
    wg = bd.transpose(1, 2, 3, 0, 4, 5).reshape(n_groups, GATE_GROUP, 4 * GATE_GROUP)
    b_all = jnp.stack([b_ra[0], b_ra[1], b_ri[0], b_ri[1]])
    bg = b_all.reshape(4, n_groups, GATE_GROUP).transpose(1, 0, 2).reshape(n_groups, 1, 4 * GATE_GROUP)
    return wg.astype(BF16), bg


def kernel(x_prompt, x_sample, state_lru, c, c_ctx, w_ada, b_ada, g_pre_mix, g_post_mix, g_pre_mlp, g_post_mlp, w_in, g_sgu, w_sp, b_sp, conv_w, conv_b, w_ra, b_ra, w_ri, b_ri, lam, w_br_g, w_br_r, w_out, w_ff1, w_ff2):
    batch, seq, dm = x_prompt.shape
    dec_batch, dec_seq, _ = x_sample.shape
    depth = w_ada.shape[0]
    tm = 256
    assert seq == tm and dec_seq % tm == 0 and SUBLANES - 1 >= dec_batch

    c_rows = jnp.concatenate(
        [c_ctx[None, :], c, jnp.zeros((SUBLANES - 1 - dec_batch, dm), F32)], axis=0)
    yp = x_prompt.reshape(batch * seq, dm)
    ys = x_sample.reshape(dec_batch * dec_seq, dm)
    zeros_sum = jnp.zeros((batch, 1, 4, dm), F32)
    zeros_h0 = jnp.zeros((batch, 2, dm), F32)
    states = []
    for l in range(depth):
        wg, bg = _gate_weights(w_ra[l], w_ri[l], b_ra[l], b_ri[l])
        p = {
            "g_pre_mix": g_pre_mix[l][None, :], "g_post_mix": g_post_mix[l][None, :],
            "g_pre_mlp": g_pre_mlp[l][None, :], "g_post_mlp": g_post_mlp[l][None, :],
            "g_sgu": g_sgu[l][None, :],
            "w_in": w_in[l].astype(BF16), "w_sp": w_sp[l].astype(BF16),
            "b_sp": jnp.broadcast_to(b_sp[l][:, :, None], (G_HEADS, CHUNK, CHUNK)),
            "conv_w": conv_w[l], "conv_b": conv_b[l][None, :],
            "wg": wg, "bg": bg, "lam": lam[l],
            "w_br_g": w_br_g[l].astype(BF16), "w_br_r": w_br_r[l].astype(BF16),
            "w_out": w_out[l].astype(BF16),
            "w_ff1": w_ff1[l].astype(BF16), "w_ff2": w_ff2[l].astype(BF16),
        }
        mod = _ada(c_rows, w_ada[l], b_ada[l][None, :])
        mod3 = mod.reshape(SUBLANES, 1, N_MOD * dm)

        yp, st = _mix(yp, mod3, zeros_sum, zeros_h0, p, tm=tm, tiles_per_seq=1, mod_row0=0,
                      per_seq_mod=False)
        yp = _mlp(yp, mod3, p, tm=tm, tiles_per_seq=1, mod_row0=0, per_seq_mod=False)
        states.append(st)

        tps = dec_seq // tm
        summaries = _pre(ys, mod3, p, tm=tm, tiles_per_seq=tps, mod_row0=1)
        h0 = state_lru[:, l].astype(F32)
        ys, _ = _mix(ys, mod3, summaries, h0, p, tm=tm, tiles_per_seq=tps, mod_row0=1,
                     per_seq_mod=True)
        ys = _mlp(ys, mod3, p, tm=tm, tiles_per_seq=tps, mod_row0=1, per_seq_mod=True)

    new_state = jnp.stack(states, axis=1).astype(x_prompt.dtype)
    return (yp.reshape(batch, seq, dm), ys.reshape(dec_batch, dec_seq, dm), new_state)
```

```python
import functools

import jax
import jax.numpy as jnp
from jax import lax
from jax.experimental import pallas as pl
from jax.experimental.pallas import tpu as pltpu

F32 = jnp.float32
BF16 = jnp.bfloat16

SUBLANES = 8
LANES = 128
BF16_ROWS = 16
VMEM_LIMIT_BYTES = 56 * 1024 * 1024

CHUNK = 128
G_HEADS = 8
LRU_HEADS = 16
LRU_DIM = 64
CONV_W = 4
CONV_PAD_L = 2
LRU_C = 8.0
N_MOD = 6
EPS = 1e-6
LOG2_E = 1.4426950408889634
LN_2 = 0.6931471805599453
HALO = BF16_ROWS
HEADS_PER_GROUP = 4
GATE_GROUP = HEADS_PER_GROUP * LRU_DIM


def _sigmoid(x):
    return 0.5 * jnp.tanh(0.5 * x) + 0.5


def _gelu2(x):
    z = x * (0.7978845608028654 * 0.044715 * (x * x) + 0.7978845608028654)
    return x * jnp.tanh(z) + x


def _tanh_gate(x_half):
    return jnp.tanh(x_half) + 1.0


def _rms_scale(x, eps=EPS):
    return x * lax.rsqrt(jnp.mean(x * x, axis=-1, keepdims=True) + eps)


def _dot(a, b):
    return jnp.dot(a, b, preferred_element_type=F32)


def _ada_kernel(c_ref, w_ref, b_ref, o_ref):
    c = c_ref[...]
    s = c * _sigmoid(c)
    s_hi = s.astype(BF16)
    s_lo = (s - s_hi.astype(F32)).astype(BF16)
    w = w_ref[...]
    w_hi = w.astype(BF16)
    w_lo = (w - w_hi.astype(F32)).astype(BF16)
    o_ref[...] = _dot(s_hi, w_hi) + _dot(s_lo, w_hi) + _dot(s_hi, w_lo) + b_ref[...]


def _ada(c_rows, w_ada, b_ada):
    d, n = w_ada.shape
    bn = 1536
    return pl.pallas_call(
        _ada_kernel,
        grid=(n // bn,),
        in_specs=[
            pl.BlockSpec((SUBLANES, d), lambda j: (0, 0)),
            pl.BlockSpec((d, bn), lambda j: (0, j)),
            pl.BlockSpec((1, bn), lambda j: (0, j)),
        ],
        out_specs=pl.BlockSpec((SUBLANES, bn), lambda j: (0, j)),
        out_shape=jax.ShapeDtypeStruct((SUBLANES, n), F32),
        compiler_params=pltpu.CompilerParams(
            dimension_semantics=("arbitrary",), vmem_limit_bytes=VMEM_LIMIT_BYTES),
        name="ada_mod",
    )(c_rows, w_ada, b_ada)


def _modulated_norm_ext(x_ref, xp_ref, xn_ref, g_ref, sh, sc, first, last, hext_ref):
    tm = x_ref.shape[0]
    scale = g_ref[...] * (1.0 + sc)

    def norm(x):
        return _rms_scale(x) * scale + sh

    hext_ref[HALO:HALO + tm, :] = norm(x_ref[...]).astype(BF16)
    if xp_ref is not None:
        hp = jnp.where(first, 0.0, norm(xp_ref[...]))
        hn = jnp.where(last, 0.0, norm(xn_ref[...]))
        hext_ref[0:HALO, :] = hp.astype(BF16)
        hext_ref[HALO + tm:HALO + tm + HALO, :] = hn.astype(BF16)


def _scan_geometry(tm):
    seg = tm // SUBLANES
    assert seg % SUBLANES == 0
    pitch = seg if (seg // SUBLANES) % 2 else seg + SUBLANES
    return seg, pitch


def _store_time_major(ref, idx, val):
    seg = val.shape[0] // SUBLANES
    for s in range(SUBLANES):
        ref[idx + (pl.ds(s, seg, stride=SUBLANES), slice(None))] = val[s * seg:(s + 1) * seg, :]


def _load_subseq(ref, idx, tm):
    seg, pitch = _scan_geometry(tm)
    return jnp.concatenate(
        [ref[idx + (pl.ds(s * pitch, seg), slice(None))] for s in range(SUBLANES)], axis=0)


def _project_and_conv(hext_ref, wxr, cw_ref, cb_ref, xr_ref, xc_ref, with_halo):
    tm = xc_ref.shape[0]
    if with_halo:
        xr = _dot(hext_ref[...], wxr)
        for lb in range(xr_ref.shape[0]):
            xr_ref[lb] = xr[:, lb * LANES:(lb + 1) * LANES]
    else:
        xr = _dot(hext_ref[HALO:HALO + tm, :], wxr)
        zeros = jnp.zeros((HALO, LANES), F32)
        for lb in range(xr_ref.shape[0]):
            xr_ref[lb, 0:HALO, :] = zeros
            xr_ref[lb, HALO:HALO + tm, :] = xr[:, lb * LANES:(lb + 1) * LANES]
            xr_ref[lb, HALO + tm:HALO + tm + HALO, :] = zeros
    for lb in range(xr_ref.shape[0]):
        cols = slice(lb * LANES, (lb + 1) * LANES)
        acc = cb_ref[:, cols]
        for k in range(CONV_W):
            off = HALO + k - CONV_PAD_L
            acc = acc + xr_ref[lb, pl.ds(off, tm), :] * cw_ref[k:k + 1, cols]
        xc_ref[:, cols] = acc


def _decay_coeff(lam_ref):
    nl = -lam_ref[...]
    softplus = jnp.maximum(nl, 0.0) + jnp.log1p(jnp.exp(-jnp.abs(nl)))
    return -LRU_C * softplus


def _sublane_prefix(a, b, reverse):
    row = lax.broadcasted_iota(jnp.int32, a.shape, 0)
    for d in (1, 2, 4):
        if reverse:
            shift, valid = SUBLANES - d, row < SUBLANES - d
        else:
            shift, valid = d, row >= d
        a_sh = jnp.where(valid, pltpu.roll(a, shift, 0), 1.0)
        b_sh = jnp.where(valid, pltpu.roll(b, shift, 0), 0.0)
        b = a * b_sh + b
        a = a * a_sh
    return a, b


def _gates_to_scan_inputs(xc_ref, wg_ref, bg_ref, coef, a_ref, b_ref, interleave=()):
    n_groups = wg_ref.shape[0]
    lb_per_group = GATE_GROUP // LANES
    for g in range(n_groups):
        if g < len(interleave):
            interleave[g]()
        cols = slice(g * GATE_GROUP, (g + 1) * GATE_GROUP)
        xc = xc_ref[:, cols]
        th = jnp.tanh(_dot(xc.astype(BF16), wg_ref[g]) + bg_ref[g])
        xc_half = 0.5 * xc
        for d in range(2):
            t_r = th[:, d * GATE_GROUP:(d + 1) * GATE_GROUP]
            t_i = th[:, (2 + d) * GATE_GROUP:(3 + d) * GATE_GROUP]
            k = (0.5 * LOG2_E) * coef[d:d + 1, cols]
            log2_a = t_r * k + k
            a = jnp.exp2(log2_a)
            t = jnp.tanh(LN_2 * log2_a)
            m = jnp.maximum((-2.0 * t) / (1.0 - t), 1e-12)
            bx = (m * lax.rsqrt(m)) * ((t_i + 1.0) * xc_half)
            for j in range(lb_per_group):
                lb = g * lb_per_group + j
                _store_time_major(a_ref, (d, lb), a[:, j * LANES:(j + 1) * LANES])
                _store_time_major(b_ref, (d, lb), bx[:, j * LANES:(j + 1) * LANES])


def _scan_tile(a_ref, b_ref, d, h0, h_ref, tm):
    seg, pitch = _scan_geometry(tm)
    n_lb = a_ref.shape[1]
    reverse = d == 1
    steps = [seg - 1 - i if reverse else i for i in range(seg)]

    ends = []
    for lb in range(n_lb):
        h = jnp.zeros((SUBLANES, LANES), F32)
        p = jnp.ones((SUBLANES, LANES), F32)
        for j in steps:
            a = a_ref[d, lb, pl.ds(j * SUBLANES, SUBLANES), :]
            h = a * h + b_ref[d, lb, pl.ds(j * SUBLANES, SUBLANES), :]
            p = a * p
        ends.append((h, p))

    row = lax.broadcasted_iota(jnp.int32, (SUBLANES, LANES), 0)
    first_row = SUBLANES - 1 if reverse else 0
    last_row = 0 if reverse else SUBLANES - 1
    shift_in = SUBLANES - 1 if reverse else 1
    inits, tile_end, tile_prod = [], [], []
    for lb in range(n_lb):
        h_end, p_end = ends[lb]
        pp, hh = _sublane_prefix(p_end, h_end, reverse)
        c0 = jnp.broadcast_to(h0[:, lb * LANES:(lb + 1) * LANES], (SUBLANES, LANES))
        full = pp * c0 + hh
        inits.append(jnp.where(row == first_row, c0, pltpu.roll(full, shift_in, 0)))
        tile_end.append(full[last_row:last_row + 1, :])
        tile_prod.append(pp[last_row:last_row + 1, :])

    if h_ref is not None:
        for lb in range(n_lb):
            h = inits[lb]
            for j in steps:
                rows = pl.ds(j * SUBLANES, SUBLANES)
                h = a_ref[d, lb, rows, :] * h + b_ref[d, lb, rows, :]
                h_ref[d, lb, pl.ds(j, SUBLANES, stride=pitch), :] = h
    return jnp.concatenate(tile_end, axis=1), jnp.concatenate(tile_prod, axis=1)


def _tile_position(tiles_per_seq):
    i = pl.program_id(0)
    k = i % tiles_per_seq
    return k, k == 0, k == tiles_per_seq - 1


def _pre_kernel(x_ref, xp_ref, xn_ref, mod_ref, gpre_ref, wxr_ref, cw_ref, cb_ref, wg_ref, bg_ref,
                lam_ref, sum_ref, hext_ref, xr_ref, xc_ref, a_ref, b_ref, *, tiles_per_seq):
    tm, dm = x_ref.shape
    _, first, last = _tile_position(tiles_per_seq)
    mod = mod_ref[0]
    sh1, sc1 = mod[:, 0:dm], mod[:, dm:2 * dm]
    _modulated_norm_ext(x_ref, xp_ref, xn_ref, gpre_ref, sh1, sc1, first, last, hext_ref)
    _project_and_conv(hext_ref, wxr_ref[...], cw_ref, cb_ref, xr_ref, xc_ref, True)
    _gates_to_scan_inputs(xc_ref, wg_ref, bg_ref, _decay_coeff(lam_ref), a_ref, b_ref)
    zero = jnp.zeros((1, dm), F32)
    for d in range(2):
        h_end, a_prod = _scan_tile(a_ref, b_ref, d, zero, None, tm)
        sum_ref[0, 0, 2 * d:2 * d + 1, :] = a_prod
        sum_ref[0, 0, 2 * d + 1:2 * d + 2, :] = h_end


def _mix_kernel(x_ref, *refs, tiles_per_seq):
    with_halo = tiles_per_seq > 1
    xp_ref, xn_ref = refs[:2] if with_halo else (None, None)
    (mod_ref, gpre_ref, gpost_ref, gsgu_ref, win_ref, wsp_ref, bsp_ref, cw_ref, cb_ref, wg_ref,
     bg_ref, lam_ref, wbg_ref, wbr_ref, wout_ref, sum_ref, h0_ref, o_ref, st_ref,
     hext_ref, gu_ref, vn_ref, yg_ref, xr_ref, xc_ref, a_ref, b_ref, h_ref, yr_ref,
     gr_ref, gb_ref, mg_ref) = refs[2:] if with_halo else refs
    tm, dm = x_ref.shape
    k, first, last = _tile_position(tiles_per_seq)
    mod = mod_ref[0]
    sh1, sc1, g1 = mod[:, 0:dm], mod[:, dm:2 * dm], mod[:, 2 * dm:3 * dm]
    _modulated_norm_ext(x_ref, xp_ref, xn_ref, gpre_ref, sh1, sc1, first, last, hext_ref)
    rows = slice(HALO, HALO + tm)

    def col(j):
        return slice(j * dm, (j + 1) * dm)

    _project_and_conv(hext_ref, win_ref[:, col(2)], cw_ref, cb_ref, xr_ref, xc_ref, with_halo)

    def project_u():
        gu_ref[...] = _gelu2(_dot(hext_ref[rows, :], win_ref[:, col(0)]))

    def project_v():
        v2 = _gelu2(_dot(hext_ref[rows, :], win_ref[:, col(1)]))
        vn_ref[...] = (_rms_scale(v2, 4.0 * EPS) * gsgu_ref[...]).astype(BF16)

    def project_gr():
        gr_ref[...] = _gelu2(_dot(hext_ref[rows, :], win_ref[:, col(3)]))

    def project_gb():
        gb_ref[...] = _tanh_gate(_dot(hext_ref[rows, :], win_ref[:, col(5)]))

    _gates_to_scan_inputs(xc_ref, wg_ref, bg_ref, _decay_coeff(lam_ref), a_ref, b_ref,
                          interleave=(project_u, project_v, project_gr, project_gb))

    for n in range(tm // CHUNK):
        for g in range(G_HEADS):
            rs = slice(n * CHUNK, (n + 1) * CHUNK)
            cs = slice(g * CHUNK, (g + 1) * CHUNK)
            s = _dot(wsp_ref[g], vn_ref[rs, cs]) + bsp_ref[g]
            yg_ref[rs, cs] = (gu_ref[rs, cs] * s).astype(BF16)
    ga = _tanh_gate(_dot(hext_ref[rows, :], win_ref[:, col(4)]))
    mg_ref[...] = ga * _dot(yg_ref[...], wbg_ref[...])

    hf0 = h0_ref[0, 0:1, :]
    hb0 = h0_ref[0, 1:2, :]
    for j in range(tiles_per_seq - 1):
        hf0 = jnp.where(j < k, sum_ref[0, j, 0:1, :] * hf0 + sum_ref[0, j, 1:2, :], hf0)
    for j in range(tiles_per_seq - 1, 0, -1):
        hb0 = jnp.where(j > k, sum_ref[0, j, 2:3, :] * hb0 + sum_ref[0, j, 3:4, :], hb0)
    hf_end, _ = _scan_tile(a_ref, b_ref, 0, hf0, h_ref, tm)
    hb_end, _ = _scan_tile(a_ref, b_ref, 1, hb0, h_ref, tm)
    st_ref[0, 0:1, :] = hf_end
    st_ref[0, 1:2, :] = hb_end
    for lb in range(h_ref.shape[1]):
        cols = slice(lb * LANES, (lb + 1) * LANES)
        h_sum = _load_subseq(h_ref, (0, lb), tm) + _load_subseq(h_ref, (1, lb), tm)
        yr_ref[:, cols] = (h_sum * gr_ref[:, cols]).astype(BF16)

    merged = mg_ref[...] + gb_ref[...] * _dot(yr_ref[...], wbr_ref[...])
    o = _dot(merged.astype(BF16), wout_ref[...])
    o_ref[...] = x_ref[...] + g1 * (_rms_scale(o) * gpost_ref[...])


def _mlp_kernel(x_ref, mod_ref, gpre_ref, gpost_ref, w1_ref, w2_ref, o_ref):
    dm = x_ref.shape[1]
    mod = mod_ref[0]
    sh2, sc2, g2 = mod[:, 3 * dm:4 * dm], mod[:, 4 * dm:5 * dm], mod[:, 5 * dm:6 * dm]
    x = x_ref[...]
    h = (_rms_scale(x) * (gpre_ref[...] * (1.0 + sc2)) + sh2).astype(BF16)
    f1 = jnp.maximum(_dot(h, w1_ref[...]), 0.0)
    f = _dot((f1 * f1).astype(BF16), w2_ref[...])
    o_ref[...] = x + g2 * (_rms_scale(f) * gpost_ref[...])


def _const_spec(shape):
    nd = len(shape)
    return pl.BlockSpec(shape, lambda i: (0,) * nd, pipeline_mode=pl.Buffered(1))


def _halo_specs(tm, dm, n_rows):
    per_tile = tm // HALO
    n_blocks = n_rows // HALO
    prev = pl.BlockSpec((HALO, dm), lambda i: (jnp.maximum(i * per_tile - 1, 0), 0))
    nxt = pl.BlockSpec((HALO, dm), lambda i: (jnp.minimum((i + 1) * per_tile, n_blocks - 1), 0))
    return prev, nxt


def _mod_spec(n_mod_cols, tiles_per_seq, mod_row0, per_seq_mod):
    if per_seq_mod:
        return pl.BlockSpec((1, 1, n_mod_cols), lambda i: (mod_row0 + i // tiles_per_seq, 0, 0))
    return pl.BlockSpec((1, 1, n_mod_cols), lambda i: (mod_row0, 0, 0))


def _scan_input_scratch(tm, dm):
    return pltpu.VMEM((2, dm // LANES, tm, LANES), F32)


def _scan_state_scratch(tm, dm):
    _, pitch = _scan_geometry(tm)
    return pltpu.VMEM((2, dm // LANES, SUBLANES * pitch, LANES), F32)


def _pre(x2d, mod3, p, *, tm, tiles_per_seq, mod_row0):
    n_rows, dm = x2d.shape
    n_tiles = n_rows // tm
    n_seq = n_tiles // tiles_per_seq
    prev, nxt = _halo_specs(tm, dm, n_rows)
    ext = tm + 2 * HALO
    return pl.pallas_call(
        functools.partial(_pre_kernel, tiles_per_seq=tiles_per_seq),
        grid=(n_tiles,),
        in_specs=[
            pl.BlockSpec((tm, dm), lambda i: (i, 0)), prev, nxt,
            _mod_spec(mod3.shape[2], tiles_per_seq, mod_row0, True),
            _const_spec((1, dm)),
            pl.BlockSpec((dm, dm), lambda i: (0, 2), pipeline_mode=pl.Buffered(1)),
            _const_spec(p["conv_w"].shape), _const_spec((1, dm)),
            _const_spec(p["wg"].shape), _const_spec(p["bg"].shape), _const_spec(p["lam"].shape),
        ],
        out_specs=pl.BlockSpec((1, 1, 4, dm), lambda i: (i // tiles_per_seq, i % tiles_per_seq, 0, 0)),
        out_shape=jax.ShapeDtypeStruct((n_seq, tiles_per_seq, 4, dm), F32),
        scratch_shapes=[
            pltpu.VMEM((ext, dm), BF16),
            pltpu.VMEM((dm // LANES, ext, LANES), F32),
            pltpu.VMEM((tm, dm), F32),
            _scan_input_scratch(tm, dm), _scan_input_scratch(tm, dm),
        ],
        compiler_params=pltpu.CompilerParams(
            dimension_semantics=("arbitrary",), vmem_limit_bytes=VMEM_LIMIT_BYTES),
        name="lru_summaries",
    )(x2d, x2d, x2d, mod3, p["g_pre_mix"], p["w_in"], p["conv_w"], p["conv_b"], p["wg"], p["bg"],
      p["lam"])


def _mix(x2d, mod3, summaries, h0, p, *, tm, tiles_per_seq, mod_row0, per_seq_mod):
    n_rows, dm = x2d.shape
    n_tiles = n_rows // tm
    halo_specs = list(_halo_specs(tm, dm, n_rows)) if tiles_per_seq > 1 else []
    halo_args = [x2d] * len(halo_specs)
    ext = tm + 2 * HALO
    return pl.pallas_call(
        functools.partial(_mix_kernel, tiles_per_seq=tiles_per_seq),
        grid=(n_tiles,),
        in_specs=[
            pl.BlockSpec((tm, dm), lambda i: (i, 0)), *halo_specs,
            _mod_spec(mod3.shape[2], tiles_per_seq, mod_row0, per_seq_mod),
            _const_spec((1, dm)), _const_spec((1, dm)), _const_spec((1, dm)),
            _const_spec(p["w_in"].shape), _const_spec(p["w_sp"].shape), _const_spec(p["b_sp"].shape),
            _const_spec(p["conv_w"].shape), _const_spec((1, dm)),
            _const_spec(p["wg"].shape), _const_spec(p["bg"].shape), _const_spec(p["lam"].shape),
            _const_spec((dm, dm)), _const_spec((dm, dm)), _const_spec((dm, dm)),
            pl.BlockSpec((1, tiles_per_seq, 4, dm), lambda i: (i // tiles_per_seq, 0, 0, 0)),
            pl.BlockSpec((1, 2, dm), lambda i: (i // tiles_per_seq, 0, 0)),
        ],
        out_specs=[
            pl.BlockSpec((tm, dm), lambda i: (i, 0)),
            pl.BlockSpec((1, 2, dm), lambda i: (i, 0, 0)),
        ],
        out_shape=[
            jax.ShapeDtypeStruct((n_rows, dm), F32),
            jax.ShapeDtypeStruct((n_tiles, 2, dm), F32),
        ],
        scratch_shapes=[
            pltpu.VMEM((ext, dm), BF16),
            pltpu.VMEM((tm, dm), F32),
            pltpu.VMEM((tm, dm), BF16),
            pltpu.VMEM((tm, dm), BF16),
            pltpu.VMEM((dm // LANES, ext, LANES), F32),
            pltpu.VMEM((tm, dm), F32),
            _scan_input_scratch(tm, dm),
            _scan_input_scratch(tm, dm),
            _scan_state_scratch(tm, dm),
            pltpu.VMEM((tm, dm), BF16),
            pltpu.VMEM((tm, dm), F32),
            pltpu.VMEM((tm, dm), F32),
            pltpu.VMEM((tm, dm), F32),
        ],
        compiler_params=pltpu.CompilerParams(
            dimension_semantics=("arbitrary",), vmem_limit_bytes=VMEM_LIMIT_BYTES),
        name="token_mix",
    )(x2d, *halo_args, mod3, p["g_pre_mix"], p["g_post_mix"], p["g_sgu"], p["w_in"], p["w_sp"],
      p["b_sp"], p["conv_w"], p["conv_b"], p["wg"], p["bg"], p["lam"], p["w_br_g"], p["w_br_r"],
      p["w_out"], summaries, h0)


def _mlp(x2d, mod3, p, *, tm, tiles_per_seq, mod_row0, per_seq_mod):
    n_rows, dm = x2d.shape
    return pl.pallas_call(
        _mlp_kernel,
        grid=(n_rows // tm,),
        in_specs=[
            pl.BlockSpec((tm, dm), lambda i: (i, 0)),
            _mod_spec(mod3.shape[2], tiles_per_seq, mod_row0, per_seq_mod),
            _const_spec((1, dm)), _const_spec((1, dm)),
            _const_spec(p["w_ff1"].shape), _const_spec(p["w_ff2"].shape),
        ],
        out_specs=pl.BlockSpec((tm, dm), lambda i: (i, 0)),
        out_shape=jax.ShapeDtypeStruct((n_rows, dm), F32),
        compiler_params=pltpu.CompilerParams(
            dimension_semantics=("arbitrary",), vmem_limit_bytes=VMEM_LIMIT_BYTES),
        name="relu2_mlp",
    )(x2d, mod3, p["g_pre_mlp"], p["g_post_mlp"], p["w_ff1"], p["w_ff2"])


def _gate_weights(w_ra, w_ri, b_ra, b_ri):
    n_groups = LRU_HEADS // HEADS_PER_GROUP
    w_all = jnp.stack([w_ra[0], w_ra[1], w_ri[0], w_ri[1]])
    w_all = w_all.reshape(4, n_groups, HEADS_PER_GROUP, LRU_DIM, LRU_DIM)
    eye = jnp.eye(HEADS_PER_GROUP, dtype=w_all.dtype)
    bd = w_all[:, :, :, :, None, :] * eye[None, None, :, None, :, None]
    wg = bd.transpose(1, 2, 3, 0, 4, 5).reshape(n_groups, GATE_GROUP, 4 * GATE_GROUP)
    b_all = jnp.stack([b_ra[0], b_ra[1], b_ri[0], b_ri[1]])
    bg = b_all.reshape(4, n_groups, GATE_GROUP).transpose(1, 0, 2).reshape(n_groups, 1, 4 * GATE_GROUP)
    return wg, bg


def kernel(x_prompt, x_sample, state_lru, c, c_ctx, w_ada, b_ada, g_pre_mix, g_post_mix, g_pre_mlp, g_post_mlp, w_in, g_sgu, w_sp, b_sp, conv_w, conv_b, w_ra, b_ra, w_ri, b_ri, lam, w_br_g, w_br_r, w_out, w_ff1, w_ff2):
    batch, seq, dm = x_prompt.shape
    dec_batch, dec_seq, _ = x_sample.shape
    depth = w_ada.shape[0]
    tm = 256
    assert seq == tm and dec_seq % tm == 0 and SUBLANES - 1 >= dec_batch

    c_rows = jnp.concatenate(
        [c_ctx[None, :], c, jnp.zeros((SUBLANES - 1 - dec_batch, dm), F32)], axis=0)
    yp = x_prompt.reshape(batch * seq, dm)
    ys = x_sample.reshape(dec_batch * dec_seq, dm)
    zeros_sum = jnp.zeros((batch, 1, 4, dm), F32)
    zeros_h0 = jnp.zeros((batch, 2, dm), F32)
    states = []
    for l in range(depth):
        wg, bg = _gate_weights(w_ra[l], w_ri[l], b_ra[l], b_ri[l])
        merge_gate_scale = jnp.concatenate(
            [jnp.ones((4 * dm,), F32), jnp.full((2 * dm,), 0.5, F32)])
        p = {
            "g_pre_mix": g_pre_mix[l][None, :], "g_post_mix": g_post_mix[l][None, :],
            "g_pre_mlp": g_pre_mlp[l][None, :], "g_post_mlp": g_post_mlp[l][None, :],
            "g_sgu": g_sgu[l][None, :],
            "w_in": (w_in[l] * merge_gate_scale).astype(BF16),
            "w_sp": (0.5 * w_sp[l]).astype(BF16),
            "b_sp": jnp.broadcast_to(0.5 * b_sp[l][:, :, None], (G_HEADS, CHUNK, CHUNK)),
            "conv_w": conv_w[l], "conv_b": conv_b[l][None, :],
            "wg": (0.5 * wg).astype(BF16), "bg": 0.5 * bg, "lam": lam[l],
            "w_br_g": (0.5 * w_br_g[l]).astype(BF16), "w_br_r": (0.25 * w_br_r[l]).astype(BF16),
            "w_out": w_out[l].astype(BF16),
            "w_ff1": w_ff1[l].astype(BF16), "w_ff2": w_ff2[l].astype(BF16),
        }
        mod = _ada(c_rows, w_ada[l], b_ada[l][None, :])
        mod3 = mod.reshape(SUBLANES, 1, N_MOD * dm)

        yp, st = _mix(yp, mod3, zeros_sum, zeros_h0, p, tm=tm, tiles_per_seq=1, mod_row0=0,
                      per_seq_mod=False)
        yp = _mlp(yp, mod3, p, tm=tm, tiles_per_seq=1, mod_row0=0, per_seq_mod=False)
        states.append(st)

        tps = dec_seq // tm
        summaries = _pre(ys, mod3, p, tm=tm, tiles_per_seq=tps, mod_row0=1)
        h0 = state_lru[:, l].astype(F32)
        ys, _ = _mix(ys, mod3, summaries, h0, p, tm=tm, tiles_per_seq=tps, mod_row0=1,
                     per_seq_mod=True)
        ys = _mlp(ys, mod3, p, tm=tm, tiles_per_seq=tps, mod_row0=1, per_seq_mod=True)

    new_state = jnp.stack(states, axis=1).astype(x_prompt.dtype)
    return (yp.reshape(batch, seq, dm), ys.reshape(dec_batch, dec_seq, dm), new_state)
```
